```python
import jax, jax.numpy as jnp
from jax import lax
import numpy as np

D_MODEL = 2048
BATCH = 2
SEQ = 4096
DEPTH = 4

D_A = D_MODEL // 2
D_B = D_MODEL // 2
D_C = D_MODEL // 2
GROUP = 128
N_GROUPS_B = D_B // GROUP
CHUNK = 128
CONV_A = 3
CONV_C = 31
N_BRANCH = 3
D_FF = -(-8 * D_MODEL // (3 * 256)) * 256
LN_EPS = 1e-5
DEEPNORM_ALPHA = (2 * DEPTH) ** 0.25
DEEPNORM_BETA = (8 * DEPTH) ** -0.25
D_IN = 3 * D_A + 2 * D_B + 2 * D_C + N_BRANCH * D_MODEL
SPLITS = (D_A, 2 * D_A, 3 * D_A, 3 * D_A + D_B, 3 * D_A + 2 * D_B,
          3 * D_A + 2 * D_B + D_C, 3 * D_A + 2 * D_B + 2 * D_C)

kernel_name = "hybrid_gated_conv_sgu_conformer_deepnorm"


def layer_norm(x, g, b):
    xf = x.astype(jnp.float32)
    mu = jnp.mean(xf, axis=-1, keepdims=True)
    var = jnp.mean(jnp.square(xf - mu), axis=-1, keepdims=True)
    y = (xf - mu) * lax.rsqrt(var + LN_EPS)
    return (y * g.astype(jnp.float32) + b.astype(jnp.float32)).astype(x.dtype)


def causal_depthwise_conv(x, w):
    k, c = w.shape
    return lax.conv_general_dilated(
        x, w[:, None, :].astype(x.dtype), window_strides=(1,), padding=[(k - 1, 0)],
        dimension_numbers=('NWC', 'WIO', 'NWC'), feature_group_count=c)


def short_gated_conv(b_gate, c_gate, h, conv_w):
    return b_gate * causal_depthwise_conv(c_gate * h, conv_w)


def chunked_spatial_gating(u, v, ln_g, ln_b, w_s, b_s):
    bsz, s, _ = v.shape
    n_chunks = s // CHUNK
    v = layer_norm(v, ln_g, ln_b).reshape(bsz, n_chunks, CHUNK, N_GROUPS_B, GROUP)
    causal = jnp.tril(jnp.ones((CHUNK, CHUNK), dtype=bool))
    w = jnp.where(causal, w_s, 0).astype(v.dtype)
    mixed = jnp.einsum('gts,bnsgd->bntgd', w, v) + b_s.T.astype(v.dtype)[:, :, None]
    return u * mixed.reshape(bsz, s, D_B)


def conformer_conv(a, gate, conv_w, conv_b, ln_g, ln_b):
    y = a * jax.nn.sigmoid(gate)
    y = causal_depthwise_conv(y, conv_w) + conv_b
    return jax.nn.silu(layer_norm(y, ln_g, ln_b))


def setup_inputs(seed: int = 0) -> dict:
    key = jax.random.key(seed)
    ks = jax.random.split(key, 24)

    def nrm(k, shape, scale):
        return jax.random.normal(k, shape, jnp.float32) * scale

    L = DEPTH
    return {
        "x": nrm(ks[0], (BATCH, SEQ, D_MODEL), 1.0),
        "ln_in_g": 1.0 + nrm(ks[1], (D_MODEL,), 0.02),
        "ln_in_b": nrm(ks[2], (D_MODEL,), 0.02),
        "w_in": nrm(ks[3], (L, D_MODEL, D_IN), D_MODEL ** -0.5),
        "gate_bias": nrm(ks[4], (L, N_BRANCH * D_MODEL), 0.02),
        "conv_a_w": nrm(ks[5], (L, CONV_A, D_A), CONV_A ** -0.5),
        "sg_ln_g": 1.0 + nrm(ks[6], (L, D_B), 0.02),
        "sg_ln_b": nrm(ks[7], (L, D_B), 0.02),
        "sg_w": nrm(ks[8], (L, N_GROUPS_B, CHUNK, CHUNK), CHUNK ** -0.5),
        "sg_b": 1.0 + nrm(ks[9], (L, N_GROUPS_B, CHUNK), 0.02),
        "cc_conv_w": nrm(ks[10], (L, CONV_C, D_C), CONV_C ** -0.5),
        "cc_conv_b": nrm(ks[11], (L, D_C), 0.02),
        "cc_ln_g": 1.0 + nrm(ks[12], (L, D_C), 0.02),
        "cc_ln_b": nrm(ks[13], (L, D_C), 0.02),
        "w_branch": nrm(ks[14], (L, N_BRANCH, D_A, D_MODEL), D_A ** -0.5),
        "w_out": nrm(ks[15], (L, D_MODEL, D_MODEL), DEEPNORM_BETA * D_MODEL ** -0.5),
        "ln_mix_g": 1.0 + nrm(ks[16], (L, D_MODEL), 0.02),
        "ln_mix_b": nrm(ks[17], (L, D_MODEL), 0.02),
        "w_ffn_in": nrm(ks[18], (L, D_MODEL, 2 * D_FF), D_MODEL ** -0.5),
        "w_ffn_out": nrm(ks[19], (L, D_FF, D_MODEL), DEEPNORM_BETA * D_FF ** -0.5),
        "ln_ffn_g": 1.0 + nrm(ks[20], (L, D_MODEL), 0.02),
        "ln_ffn_b": nrm(ks[21], (L, D_MODEL), 0.02),
    }


def reference(x, ln_in_g, ln_in_b, w_in, gate_bias, conv_a_w, sg_ln_g, sg_ln_b, sg_w, sg_b,
              cc_conv_w, cc_conv_b, cc_ln_g, cc_ln_b, w_branch, w_out, ln_mix_g, ln_mix_b,
              w_ffn_in, w_ffn_out, ln_ffn_g, ln_ffn_b):
    bsz, s, _ = x.shape
    x = layer_norm(x, ln_in_g, ln_in_b)
    for l in range(DEPTH):
        z = jnp.einsum('bsd,de->bse', x, w_in[l])
        a_b, a_c, a_h, b_u, b_v, c_a, c_g, g = jnp.split(z, SPLITS, axis=-1)
        gates = jax.nn.sigmoid(g + gate_bias[l]).reshape(bsz, s, N_BRANCH, D_MODEL)
        y_a = short_gated_conv(a_b, a_c, a_h, conv_a_w[l])
        y_b = chunked_spatial_gating(b_u, b_v, sg_ln_g[l], sg_ln_b[l], sg_w[l], sg_b[l])
        y_c = conformer_conv(c_a, c_g, cc_conv_w[l], cc_conv_b[l], cc_ln_g[l], cc_ln_b[l])
        ys = jnp.stack([y_a, y_b, y_c], axis=2)
        proj = jnp.einsum('bsnc,ncd->bsnd', ys, w_branch[l])
        merged = jnp.sum(gates * proj, axis=2)
        mix = jnp.einsum('bsd,de->bse', merged, w_out[l])
        x = layer_norm(DEEPNORM_ALPHA * x + mix, ln_mix_g[l], ln_mix_b[l])
        h_gate, h_up = jnp.split(jnp.einsum('bsd,df->bsf', x, w_ffn_in[l]), 2, axis=-1)
        ffn = jnp.einsum('bsf,fd->bsd', jax.nn.silu(h_gate) * h_up, w_ffn_out[l])
        x = layer_norm(DEEPNORM_ALPHA * x + ffn, ln_ffn_g[l], ln_ffn_b[l])
    return x
```

```python
import functools

import jax
import jax.numpy as jnp
from jax import lax
from jax.experimental import pallas as pl
from jax.experimental.pallas import tpu as pltpu

LN_EPS = 1e-5
GROUP = 128
CHUNK = 128
N_BRANCH = 3

V7X_VMEM_BYTES = 64 * 2**20
LANES = 128
SUBLANES = 8
CONV_HISTORY_A = SUBLANES
CONV_HISTORY_C = 4 * SUBLANES
CONV_ROW_BLOCK = 64

BF16 = jnp.bfloat16
F32 = jnp.float32


def _dot(a, b):
    return jnp.dot(a, b, preferred_element_type=F32)


def _layer_norm(v, g, b):
    mu = jnp.mean(v, axis=-1, keepdims=True)
    c = v - mu
    var = jnp.mean(c * c, axis=-1, keepdims=True)
    return c * lax.rsqrt(var + LN_EPS) * g + b


def _nbytes(shape, dtype):
    n = jnp.dtype(dtype).itemsize
    for s in shape:
        n *= s
    return n


def _compiler_params(n_grid, blocks, scratch=(), temps=()):
    need = sum(_nbytes(s, d) * n for s, d, n in blocks)
    need += sum(_nbytes(s, d) for s, d in scratch)
    need += sum(_nbytes(s, d) for s, d in temps)
    return pltpu.CompilerParams(
        dimension_semantics=("arbitrary",) * n_grid,
        vmem_limit_bytes=min(need, V7X_VMEM_BYTES))


def _resident(shape, index_map):
    return pl.BlockSpec(shape, index_map, pipeline_mode=pl.Buffered(1))


def _ln_in_kernel(x_ref, g_ref, b_ref, of_ref, ob_ref):
    y = _layer_norm(x_ref[...], g_ref[...], b_ref[...])
    of_ref[...] = y
    ob_ref[...] = y.astype(BF16)


def _ln_in(x2d, g, b, tm):
    m, d = x2d.shape
    row = pl.BlockSpec((tm, d), lambda i: (i, 0))
    vec = _resident((1, d), lambda i: (0, 0))
    return pl.pallas_call(
        _ln_in_kernel,
        grid=(m // tm,),
        in_specs=[row, vec, vec],
        out_specs=[row, row],
        out_shape=[jax.ShapeDtypeStruct((m, d), F32), jax.ShapeDtypeStruct((m, d), BF16)],
        compiler_params=_compiler_params(
            1, [((tm, d), F32, 2), ((tm, d), F32, 2), ((tm, d), BF16, 2)], temps=[((tm, d), F32)] * 2),
        name="ln_in",
    )(x2d, g.reshape(1, d), b.reshape(1, d))


def _branch_a_kernel(x_ref, wb_ref, wc_ref, wh_ref, cw_ref, o_ref, pbuf_ref, *, tiles_per_seq):
    i = pl.program_id(0)
    tm = x_ref.shape[0]
    h = CONV_HISTORY_A
    first = lax.rem(i, tiles_per_seq) == 0

    @pl.when(first)
    def _zero_history():
        pbuf_ref[0:h, :] = jnp.zeros((h, pbuf_ref.shape[1]), F32)

    @pl.when(jnp.logical_not(first))
    def _carry_history():
        pbuf_ref[0:h, :] = pbuf_ref[tm:tm + h, :]

    x = x_ref[...]
    pbuf_ref[h:h + tm, :] = _dot(x, wc_ref[...]) * _dot(x, wh_ref[...])
    cw = cw_ref[...]
    conv = (cw[2:3, :] * pbuf_ref[h:h + tm, :]
            + cw[1:2, :] * pbuf_ref[h - 1:h - 1 + tm, :]
            + cw[0:1, :] * pbuf_ref[h - 2:h - 2 + tm, :])
    o_ref[...] = (_dot(x, wb_ref[...]) * conv).astype(BF16)


def _branch_a(xb, w_in_b, conv_w, layer, tm, seq):
    m, d = xb.shape
    da = conv_w.shape[-1]
    wspec = lambda blk: _resident((None, d, da), lambda i: (layer, 0, blk))
    return pl.pallas_call(
        functools.partial(_branch_a_kernel, tiles_per_seq=seq // tm),
        grid=(m // tm,),
        in_specs=[pl.BlockSpec((tm, d), lambda i: (i, 0)), wspec(0), wspec(1), wspec(2),
                  _resident((None,) + conv_w.shape[1:], lambda i: (layer, 0, 0))],
        out_specs=pl.BlockSpec((tm, da), lambda i: (i, 0)),
        out_shape=jax.ShapeDtypeStruct((m, da), BF16),
        scratch_shapes=[pltpu.VMEM((tm + CONV_HISTORY_A, da), F32)],
        compiler_params=_compiler_params(
            1, [((tm, d), BF16, 2), ((d, da), BF16, 3), ((tm, da), BF16, 2)],
            scratch=[((tm + CONV_HISTORY_A, da), F32)], temps=[((tm, da), F32)] * 4),
        name="branch_a",
    )(xb, w_in_b, w_in_b, w_in_b, conv_w)


def _branch_b_kernel(x_ref, wu_ref, wv_ref, g_ref, b_ref, sw_ref, sb_ref, o_ref, vn_ref, u_ref):
    tm = x_ref.shape[0]
    n_groups = sw_ref.shape[0]
    x = x_ref[...]
    vn_ref[...] = _layer_norm(_dot(x, wv_ref[...]), g_ref[...], b_ref[...]).astype(BF16)
    u_ref[...] = _dot(x, wu_ref[...])
    t_idx = lax.broadcasted_iota(jnp.int32, (CHUNK, CHUNK), 0)
    s_idx = lax.broadcasted_iota(jnp.int32, (CHUNK, CHUNK), 1)
    causal = t_idx >= s_idx
    for g in range(n_groups):
        lanes = slice(g * GROUP, (g + 1) * GROUP)
        w = jnp.where(causal, sw_ref[g], 0.0).astype(BF16)
        bias = sb_ref[g]
        for c in range(tm // CHUNK):
            rows = slice(c * CHUNK, (c + 1) * CHUNK)
            mixed = _dot(w, vn_ref[rows, lanes]) + bias
            o_ref[rows, lanes] = (u_ref[rows, lanes] * mixed).astype(BF16)


def _branch_b(xb, w_in_b, ln_g, ln_b, sg_w, sg_bias_tile, layer, tm):
    m, d = xb.shape
    db = ln_g.shape[-1]
    n_groups = sg_w.shape[1]
    wspec = lambda blk: _resident((None, d, db), lambda i: (layer, 0, blk))
    vec = _resident((None, 1, db), lambda i: (layer, 0, 0))
    grp = _resident((None, n_groups, CHUNK, CHUNK), lambda i: (layer, 0, 0, 0))
    return pl.pallas_call(
        _branch_b_kernel,
        grid=(m // tm,),
        in_specs=[pl.BlockSpec((tm, d), lambda i: (i, 0)), wspec(3), wspec(4), vec, vec, grp, grp],
        out_specs=pl.BlockSpec((tm, db), lambda i: (i, 0)),
        out_shape=jax.ShapeDtypeStruct((m, db), BF16),
        scratch_shapes=[pltpu.VMEM((tm, db), BF16), pltpu.VMEM((tm, db), F32)],
        compiler_params=_compiler_params(
            1, [((tm, d), BF16, 2), ((d, db), BF16, 2), ((tm, db), BF16, 2),
                ((n_groups, CHUNK, CHUNK), F32, 2)],
            scratch=[((tm, db), BF16), ((tm, db), F32)], temps=[((tm, db), F32)] * 4),
        name="branch_b",
    )(xb, w_in_b, w_in_b, ln_g, ln_b, sg_w, sg_bias_tile)


def _branch_c_kernel(x_ref, wa_ref, wg_ref, cw_ref, cb_ref, g_ref, b_ref, o_ref, gbuf_ref, conv_ref,
                     *, tiles_per_seq):
    i = pl.program_id(0)
    tm = x_ref.shape[0]
    dc = gbuf_ref.shape[1]
    h = CONV_HISTORY_C
    rb = CONV_ROW_BLOCK
    n_taps = cw_ref.shape[0]
    first_off = h - (n_taps - 1)
    first = lax.rem(i, tiles_per_seq) == 0

    @pl.when(first)
    def _zero_history():
        gbuf_ref[0:h, :] = jnp.zeros((h, dc), F32)

    @pl.when(jnp.logical_not(first))
    def _carry_history():
        gbuf_ref[0:h, :] = gbuf_ref[tm:tm + h, :]

    x = x_ref[...]
    gbuf_ref[h:h + tm, :] = _dot(x, wa_ref[...]) * jax.nn.sigmoid(_dot(x, wg_ref[...]))

    def conv_rows(r, carry):
        base = pl.multiple_of(r * rb, rb)
        for lb in range(dc // LANES):
            lanes = slice(lb * LANES, (lb + 1) * LANES)
            win = gbuf_ref[pl.ds(base, rb + h), lanes]
            acc = jnp.broadcast_to(cb_ref[:, lanes], (rb, LANES))
            for s in range(SUBLANES):
                offs = [o for o in range(first_off, h + 1) if o % SUBLANES == s]
                if not offs:
                    continue
                span = offs[-1] - s + rb
                shifted = win[s:s + span, :]
                for o in offs:
                    k = o - first_off
                    acc = acc + cw_ref[k:k + 1, lanes] * shifted[o - s:o - s + rb, :]
            conv_ref[pl.ds(base, rb), lanes] = acc
        return carry

    lax.fori_loop(0, tm // rb, conv_rows, 0)
    y = _layer_norm(conv_ref[...], g_ref[...], b_ref[...])
    o_ref[...] = (y * jax.nn.sigmoid(y)).astype(BF16)


def _branch_c(xb, w_in_b, conv_w, conv_b, ln_g, ln_b, layer, tm, seq):
    m, d = xb.shape
    dc = ln_g.shape[-1]
    wspec = lambda blk: _resident((None, d, dc), lambda i: (layer, 0, blk))
    vec = _resident((None, 1, dc), lambda i: (layer, 0, 0))
    return pl.pallas_call(
        functools.partial(_branch_c_kernel, tiles_per_seq=seq // tm),
        grid=(m // tm,),
        in_specs=[pl.BlockSpec((tm, d), lambda i: (i, 0)), wspec(5), wspec(6),
                  _resident((None,) + conv_w.shape[1:], lambda i: (layer, 0, 0)), vec, vec, vec],
        out_specs=pl.BlockSpec((tm, dc), lambda i: (i, 0)),
        out_shape=jax.ShapeDtypeStruct((m, dc), BF16),
        scratch_shapes=[pltpu.VMEM((tm + CONV_HISTORY_C, dc), F32), pltpu.VMEM((tm, dc), F32)],
        compiler_params=_compiler_params(
            1, [((tm, d), BF16, 2), ((d, dc), BF16, 2), ((tm, dc), BF16, 2)],
            scratch=[((tm + CONV_HISTORY_C, dc), F32), ((tm, dc), F32)], temps=[((tm, dc), F32)] * 4),
        name="branch_c",
    )(xb, w_in_b, w_in_b, conv_w, conv_b, ln_g, ln_b)


def _merge_kernel(x_ref, ya_ref, yb_ref, yc_ref, wga_ref, wgb_ref, wgc_ref, gba_ref, gbb_ref, gbc_ref,
                  wpa_ref, wpb_ref, wpc_ref, o_ref):
    x = x_ref[...]
    acc = None
    for y_ref, wg_ref, gb_ref, wp_ref in ((ya_ref, wga_ref, gba_ref, wpa_ref),
                                          (yb_ref, wgb_ref, gbb_ref, wpb_ref),
                                          (yc_ref, wgc_ref, gbc_ref, wpc_ref)):
        gate = jax.nn.sigmoid(_dot(x, wg_ref[...]) + gb_ref[...])
        term = gate * _dot(y_ref[...], wp_ref[...])
        acc = term if acc is None else acc + term
    o_ref[...] = acc.astype(BF16)


def _merge(xb, ya, yb, yc, w_in_b, gate_bias, w_branch_b, layer, tm, tn, gate_col0):
    m, d = xb.shape
    dy = ya.shape[1]
    nj = d // tn
    xspec = pl.BlockSpec((tm, d), lambda i, j: (i, 0))
    yspec = pl.BlockSpec((tm, dy), lambda i, j: (i, 0))
    gate_w = lambda n: pl.BlockSpec((None, d, tn), lambda i, j: (layer, 0, gate_col0 // tn + n * nj + j))
    gate_b = lambda n: pl.BlockSpec((None, 1, tn), lambda i, j: (layer, 0, n * nj + j))
    proj_w = lambda n: pl.BlockSpec((None, None, dy, tn), lambda i, j: (layer, n, 0, j))
    return pl.pallas_call(
        _merge_kernel,
        grid=(m // tm, nj),
        in_specs=[xspec, yspec, yspec, yspec, gate_w(0), gate_w(1), gate_w(2),
                  gate_b(0), gate_b(1), gate_b(2), proj_w(0), proj_w(1), proj_w(2)],
        out_specs=pl.BlockSpec((tm, tn), lambda i, j: (i, j)),
        out_shape=jax.ShapeDtypeStruct((m, d), BF16),
        compiler_params=_compiler_params(
            2, [((tm, d), BF16, 2), ((tm, dy), BF16, 6), ((d, tn), BF16, 6), ((dy, tn), BF16, 6),
                ((tm, tn), BF16, 2)], temps=[((tm, tn), F32)] * 6),
        name="gate_merge",
    )(xb, ya, yb, yc, w_in_b, w_in_b, w_in_b, gate_bias, gate_bias, gate_bias,
      w_branch_b, w_branch_b, w_branch_b)


def _mm_resid_ln_kernel(a_ref, w_ref, r_ref, g_ref, b_ref, of_ref, ob_ref, acc_ref, *, alpha):
    j = pl.program_id(1)
    nj, tm, tn = acc_ref.shape
    acc_ref[j] = alpha * r_ref[...] + _dot(a_ref[...], w_ref[...])

    @pl.when(j == nj - 1)
    def _normalise():
        n = nj * tn
        total = jnp.sum(acc_ref[0], axis=-1, keepdims=True)
        for jj in range(1, nj):
            total = total + jnp.sum(acc_ref[jj], axis=-1, keepdims=True)
        mu = total / n
        sq = None
        for jj in range(nj):
            c = acc_ref[jj] - mu
            part = jnp.sum(c * c, axis=-1, keepdims=True)
            sq = part if sq is None else sq + part
        inv = lax.rsqrt(sq / n + LN_EPS)
        for jj in range(nj):
            cols = slice(jj * tn, (jj + 1) * tn)
            y = (acc_ref[jj] - mu) * inv * g_ref[:, cols] + b_ref[:, cols]
            of_ref[:, cols] = y
            ob_ref[:, cols] = y.astype(BF16)


def _mm_resid_ln(a, w_b, resid, ln_g, ln_b, layer, tm, tn, alpha, name):
    m, k = a.shape
    n = resid.shape[1]
    vec = _resident((None, 1, n), lambda i, j: (layer, 0, 0))
    row = pl.BlockSpec((tm, n), lambda i, j: (i, 0))
    return pl.pallas_call(
        functools.partial(_mm_resid_ln_kernel, alpha=alpha),
        grid=(m // tm, n // tn),
        in_specs=[pl.BlockSpec((tm, k), lambda i, j: (i, 0)),
                  pl.BlockSpec((None, k, tn), lambda i, j: (layer, 0, j)),
                  pl.BlockSpec((tm, tn), lambda i, j: (i, j)), vec, vec],
        out_specs=[row, row],
        out_shape=[jax.ShapeDtypeStruct((m, n), F32), jax.ShapeDtypeStruct((m, n), BF16)],
        scratch_shapes=[pltpu.VMEM((n // tn, tm, tn), F32)],
        compiler_params=_compiler_params(
            2, [((tm, k), BF16, 2), ((k, tn), BF16, 2), ((tm, tn), F32, 2), ((tm, n), F32, 2),
                ((tm, n), BF16, 2)], scratch=[((tm, n), F32)], temps=[((tm, tn), F32)] * 3 + [((tm, n), F32)]),
        name=name,
    )(a, w_b, resid, ln_g, ln_b)


def _swiglu_kernel(x_ref, wg_ref, wu_ref, o_ref):
    x = x_ref[...]
    hg = _dot(x, wg_ref[...])
    o_ref[...] = (hg * jax.nn.sigmoid(hg) * _dot(x, wu_ref[...])).astype(BF16)


def _swiglu(xb, w_ffn_in_b, layer, tm, tn):
    m, d = xb.shape
    dff = w_ffn_in_b.shape[-1] // 2
    nj = dff // tn
    return pl.pallas_call(
        _swiglu_kernel,
        grid=(m // tm, nj),
        in_specs=[pl.BlockSpec((tm, d), lambda i, j: (i, 0)),
                  pl.BlockSpec((None, d, tn), lambda i, j: (layer, 0, j)),
                  pl.BlockSpec((None, d, tn), lambda i, j: (layer, 0, nj + j))],
        out_specs=pl.BlockSpec((tm, tn), lambda i, j: (i, j)),
        out_shape=jax.ShapeDtypeStruct((m, dff), BF16),
        compiler_params=_compiler_params(
            2, [((tm, d), BF16, 2), ((d, tn), BF16, 4), ((tm, tn), BF16, 2)], temps=[((tm, tn), F32)] * 4),
        name="swiglu_up",
    )(xb, w_ffn_in_b, w_ffn_in_b)


def kernel(x, ln_in_g, ln_in_b, w_in, gate_bias, conv_a_w, sg_ln_g, sg_ln_b, sg_w, sg_b, cc_conv_w, cc_conv_b,
           cc_ln_g, cc_ln_b, w_branch, w_out, ln_mix_g, ln_mix_b, w_ffn_in, w_ffn_out, ln_ffn_g, ln_ffn_b):
    bsz, seq, d = x.shape
    depth = w_in.shape[0]
    d_mix = conv_a_w.shape[-1]
    alpha = (2 * depth) ** 0.25
    gate_col0 = w_in.shape[-1] - N_BRANCH * d

    w_in_b = w_in.astype(BF16)
    w_branch_b = w_branch.astype(BF16)
    w_out_b = w_out.astype(BF16)
    w_ffn_in_b = w_ffn_in.astype(BF16)
    w_ffn_out_b = w_ffn_out.astype(BF16)
    vec3 = lambda v: v.reshape(depth, 1, v.shape[-1])
    sg_bias_tile = jnp.broadcast_to(sg_b[:, :, :, None], sg_b.shape + (GROUP,))

    tm_mix = 512
    xf, xb = _ln_in(x.reshape(bsz * seq, d), ln_in_g, ln_in_b, tm_mix)
    for l in range(depth):
        ya = _branch_a(xb, w_in_b, conv_a_w, l, tm_mix, seq)
        yb = _branch_b(xb, w_in_b, vec3(sg_ln_g), vec3(sg_ln_b), sg_w, sg_bias_tile, l, tm_mix)
        yc = _branch_c(xb, w_in_b, cc_conv_w, vec3(cc_conv_b), vec3(cc_ln_g), vec3(cc_ln_b), l, tm_mix, seq)
        merged = _merge(xb, ya, yb, yc, w_in_b, vec3(gate_bias), w_branch_b, l, 1024, 256, gate_col0)
        xf, xb = _mm_resid_ln(merged, w_out_b, xf, vec3(ln_mix_g), vec3(ln_mix_b), l, 512, 1024, alpha,
                              "out_proj_ln")
        hidden = _swiglu(xb, w_ffn_in_b, l, 1024, 512)
        xf, xb = _mm_resid_ln(hidden, w_ffn_out_b, xf, vec3(ln_ffn_g), vec3(ln_ffn_b), l, 512, 512, alpha,
                              "ffn_down_ln")
    return xf.reshape(bsz, seq, d)
```
